```python
import jax, jax.numpy as jnp
from jax import lax
import numpy as np

D_MODEL = 1024
BATCH = 16
SEQ = 4096
DEPTH = 1
DEC_BATCH = 1
DEC_SEQ = 16384
PAST_LEN = 128

GRID_W = 64
MLA_HEADS = 8
QK_NOPE = 64
QK_ROPE = 32
V_HEAD = 64
Q_LORA = 768
KV_LORA = 256
ROPE_BASE = 10000.0
Q_BLOCK = 128
NA_HEADS = 8
NA_HD = 64
NA_WIN_R = 8
NA_WIN_C = 16
D_FF = 2816
CONV_W = 3
EPS = 1e-6

MLA_W = MLA_HEADS * V_HEAD
NA_W = NA_HEADS * NA_HD
SPLITS = (Q_LORA, KV_LORA, QK_ROPE, NA_W, NA_W, NA_W, 2 * D_MODEL)
IN_COLS = sum(SPLITS)

kernel_name = 'hybrid_mla_natten_convffn_encoder'


def rmsnorm(x, g):
    xf = x.astype(jnp.float32)
    y = xf * lax.rsqrt(jnp.mean(xf * xf, axis=-1, keepdims=True) + EPS)
    return (y * g.astype(jnp.float32)).astype(x.dtype)


def rope_tables(S):
    inv = ROPE_BASE ** (-jnp.arange(0, QK_ROPE, 2, dtype=jnp.float32) / QK_ROPE)
    ang = jnp.arange(S, dtype=jnp.float32)[:, None] * inv[None, :]
    return jnp.cos(ang), jnp.sin(ang)


def rotary(x, cos, sin):
    x1, x2 = jnp.split(x, 2, axis=-1)
    c = cos[None, :, None, :].astype(x.dtype)
    s = sin[None, :, None, :].astype(x.dtype)
    return jnp.concatenate([x1 * c - x2 * s, x1 * s + x2 * c], axis=-1)


def mla_branch(c_q, c_kv, k_pe, g_q, w_qb, g_kv, w_kvb):
    B, S, _ = c_q.shape
    q = (rmsnorm(c_q, g_q) @ w_qb).reshape(B, S, MLA_HEADS, QK_NOPE + QK_ROPE)
    q_nope, q_pe = q[..., :QK_NOPE], q[..., QK_NOPE:]
    kv = (rmsnorm(c_kv, g_kv) @ w_kvb).reshape(B, S, MLA_HEADS, QK_NOPE + V_HEAD)
    k_nope, v = kv[..., :QK_NOPE], kv[..., QK_NOPE:]
    cos, sin = rope_tables(S)
    q_pe = rotary(q_pe, cos, sin)
    k_pe = rotary(k_pe[:, :, None, :], cos, sin)[:, :, 0, :]
    scale = (QK_NOPE + QK_ROPE) ** -0.5
    n_blk = S // Q_BLOCK
    qn_b = q_nope.reshape(B, n_blk, Q_BLOCK, MLA_HEADS, QK_NOPE).transpose(1, 0, 2, 3, 4)
    qp_b = q_pe.reshape(B, n_blk, Q_BLOCK, MLA_HEADS, QK_ROPE).transpose(1, 0, 2, 3, 4)

    def block(args):
        qn, qp = args
        s = (jnp.einsum('bqhd,bkhd->bhqk', qn, k_nope)
             + jnp.einsum('bqhr,bkr->bhqk', qp, k_pe))
        p = jax.nn.softmax(s.astype(jnp.float32) * scale, axis=-1).astype(v.dtype)
        return jnp.einsum('bhqk,bkhd->bqhd', p, v)

    o = lax.map(block, (qn_b, qp_b))
    return o.transpose(1, 0, 2, 3, 4).reshape(B, S, MLA_W)


def na_branch(q, k, v, rpb):
    B, S, _ = q.shape
    rows = S // GRID_W
    wr = min(NA_WIN_R, rows)
    wc = NA_WIN_C
    n_keys = wr * wc
    q = q.reshape(B, rows, GRID_W, NA_HEADS, NA_HD).transpose(1, 0, 2, 3, 4)
    k = k.reshape(B, S, NA_HEADS, NA_HD)
    v = v.reshape(B, S, NA_HEADS, NA_HD)
    r = jnp.arange(rows)
    c = jnp.arange(GRID_W)
    rs = jnp.clip(r - wr // 2, 0, rows - wr)
    cs = jnp.clip(c - wc // 2, 0, GRID_W - wc)
    key_r = rs[:, None] + jnp.arange(wr)[None, :]
    key_c = cs[:, None] + jnp.arange(wc)[None, :]
    idx = (key_r[:, None, :, None] * GRID_W + key_c[None, :, None, :]).reshape(rows, GRID_W, n_keys)
    d_row = key_r - r[:, None] + (NA_WIN_R - 1)
    d_col = key_c - c[:, None] + (NA_WIN_C - 1)
    scale = NA_HD ** -0.5

    def row(args):
        q_r, idx_r, dr_r = args
        k_g = k[:, idx_r]
        v_g = v[:, idx_r]
        s = jnp.einsum('bwhd,bwkhd->bhwk', q_r, k_g).astype(jnp.float32) * scale
        bias = rpb[:, dr_r][:, :, d_col]
        bias = bias.transpose(0, 2, 1, 3).reshape(NA_HEADS, GRID_W, n_keys)
        p = jax.nn.softmax(s + bias[None].astype(jnp.float32), axis=-1).astype(v.dtype)
        return jnp.einsum('bhwk,bwkhd->bwhd', p, v_g)

    o = lax.map(row, (q, idx, d_row))
    return o.transpose(1, 0, 2, 3, 4).reshape(B, S, NA_W)


def conv_ffn(h, w_up, conv_w, conv_b, w_down):
    u, g = jnp.split(h @ w_up, 2, axis=-1)
    gp = jnp.pad(g, ((0, 0), (1, 1), (0, 0)))
    g = gp[:, :-2] * conv_w[0] + gp[:, 1:-1] * conv_w[1] + gp[:, 2:] * conv_w[2] + conv_b
    return (jax.nn.silu(g) * u) @ w_down


def trunk(x, norm_mix, w_in, b_gate, norm_q, w_qb, norm_kv, w_kvb, rpb,
          w_br_a, w_br_b, w_o, norm_ffn, w_up, conv_w, conv_b, w_down, norm_final):
    B, S, _ = x.shape
    cuts = list(np.cumsum(SPLITS)[:-1])
    for l in range(DEPTH):
        h = rmsnorm(x, norm_mix[l])
        c_q, c_kv, k_pe, na_q, na_k, na_v, gate_logits = jnp.split(h @ w_in[l], cuts, axis=-1)
        o_a = mla_branch(c_q, c_kv, k_pe, norm_q[l], w_qb[l], norm_kv[l], w_kvb[l])
        o_b = na_branch(na_q, na_k, na_v, rpb[l])
        gates = jax.nn.sigmoid(gate_logits.reshape(B, S, 2, D_MODEL) + b_gate[l])
        merged = gates[:, :, 0] * (o_a @ w_br_a[l]) + gates[:, :, 1] * (o_b @ w_br_b[l])
        x = x + merged @ w_o[l]
        x = x + conv_ffn(rmsnorm(x, norm_ffn[l]), w_up[l], conv_w[l], conv_b[l], w_down[l])
    return rmsnorm(x, norm_final)


def setup_inputs(seed: int = 0) -> dict:
    key = jax.random.key(seed)
    ks = jax.random.split(key, 24)

    def nrm(k, shape, fan):
        return jax.random.normal(k, shape, jnp.float32) * (fan ** -0.5)

    def gain(k, shape):
        return 1.0 + 0.05 * jax.random.normal(k, shape, jnp.float32)

    return {
        'x_prompt': jax.random.normal(ks[0], (BATCH, SEQ, D_MODEL), jnp.float32),
        'x_sample': jax.random.normal(ks[1], (DEC_BATCH, DEC_SEQ, D_MODEL), jnp.float32),
        'norm_mix': gain(ks[2], (DEPTH, D_MODEL)),
        'w_in': nrm(ks[3], (DEPTH, D_MODEL, IN_COLS), D_MODEL),
        'b_gate': 0.1 * jax.random.normal(ks[4], (DEPTH, 2, D_MODEL), jnp.float32),
        'norm_q': gain(ks[5], (DEPTH, Q_LORA)),
        'w_qb': nrm(ks[6], (DEPTH, Q_LORA, MLA_HEADS * (QK_NOPE + QK_ROPE)), Q_LORA),
        'norm_kv': gain(ks[7], (DEPTH, KV_LORA)),
        'w_kvb': nrm(ks[8], (DEPTH, KV_LORA, MLA_HEADS * (QK_NOPE + V_HEAD)), KV_LORA),
        'rpb': 0.1 * jax.random.normal(ks[9], (DEPTH, NA_HEADS, 2 * NA_WIN_R - 1, 2 * NA_WIN_C - 1), jnp.float32),
        'w_br_a': nrm(ks[10], (DEPTH, MLA_W, D_MODEL), MLA_W),
        'w_br_b': nrm(ks[11], (DEPTH, NA_W, D_MODEL), NA_W),
        'w_o': nrm(ks[12], (DEPTH, D_MODEL, D_MODEL), 2 * D_MODEL),
        'norm_ffn': gain(ks[13], (DEPTH, D_MODEL)),
        'w_up': nrm(ks[14], (DEPTH, D_MODEL, 2 * D_FF), D_MODEL),
        'conv_w': nrm(ks[15], (DEPTH, CONV_W, D_FF), CONV_W),
        'conv_b': 0.02 * jax.random.normal(ks[16], (DEPTH, D_FF), jnp.float32),
        'w_down': nrm(ks[17], (DEPTH, D_FF, D_MODEL), D_FF),
        'norm_final': gain(ks[18], (D_MODEL,)),
    }


def reference(x_prompt, x_sample, norm_mix, w_in, b_gate, norm_q, w_qb, norm_kv, w_kvb, rpb,
              w_br_a, w_br_b, w_o, norm_ffn, w_up, conv_w, conv_b, w_down, norm_final):
    y_prompt = trunk(x_prompt, norm_mix, w_in, b_gate, norm_q, w_qb, norm_kv, w_kvb, rpb,
                     w_br_a, w_br_b, w_o, norm_ffn, w_up, conv_w, conv_b, w_down, norm_final)
    y_sample = trunk(x_sample, norm_mix, w_in, b_gate, norm_q, w_qb, norm_kv, w_kvb, rpb,
                     w_br_a, w_br_b, w_o, norm_ffn, w_up, conv_w, conv_b, w_down, norm_final)
    return (y_prompt, y_sample)
```

```python
import functools
import math

import jax
import jax.numpy as jnp
import numpy as np
from jax import lax
from jax.experimental import pallas as pl
from jax.experimental.pallas import tpu as pltpu

D_MODEL = 1024
GRID_W = 64
MLA_HEADS = 8
QK_NOPE = 64
QK_ROPE = 32
V_HEAD = 64
Q_LORA = 768
KV_LORA = 256
ROPE_BASE = 10000.0
NA_HEADS = 8
NA_HD = 64
NA_WIN_R = 8
NA_WIN_C = 16
D_FF = 2816
EPS = 1e-6

LANES = 128
HEAD_SLOT = 128
LOG2E = 1.4426950408889634
NEG = -1e30
VMEM_LIMIT = 56 * 1024 * 1024

NA_QROWS = 4
NA_WROWS = NA_QROWS + NA_WIN_R
NA_BLK = NA_QROWS * GRID_W

BF16 = jnp.bfloat16
F32 = jnp.float32
NT_DIMS = (((1,), (1,)), ((), ()))


def _rms(x, g):
    return x * lax.rsqrt(jnp.mean(x * x, axis=-1, keepdims=True) + EPS) * g


def _const_spec(shape):
    zeros = (0,) * len(shape)
    return pl.BlockSpec(shape, lambda *_: zeros, pipeline_mode=pl.Buffered(1))


def _in_proj_kernel(x_ref, gmix_ref, w1_ref, w2_ref, w3_ref, bgate_ref, gq_ref, wqb_ref,
                    gkv_ref, wk_ref, wvt_ref, cos_ref, sina_ref, sinb_ref,
                    q_ref, k_ref, vt_ref, naq_ref, nak_ref, nav_ref, gate_ref):
    h = _rms(x_ref[...], gmix_ref[...]).astype(BF16)
    p1 = jnp.dot(h, w1_ref[...], preferred_element_type=F32)
    cq = _rms(p1[:, :Q_LORA], gq_ref[...]).astype(BF16)
    ckv = _rms(p1[:, Q_LORA:Q_LORA + KV_LORA], gkv_ref[...]).astype(BF16)
    kpe = p1[:, Q_LORA + KV_LORA:]
    cos, sina, sinb = cos_ref[...], sina_ref[...], sinb_ref[...]

    def rope(slab):
        return (slab * cos + pltpu.roll(slab, HEAD_SLOT - QK_ROPE // 2, 1) * sina
                + pltpu.roll(slab, QK_ROPE // 2, 1) * sinb)

    kpe_rot = rope(kpe)
    q = jnp.dot(cq, wqb_ref[...], preferred_element_type=F32)
    kn = jnp.dot(ckv, wk_ref[...], preferred_element_type=F32)
    q_scale = (QK_NOPE + QK_ROPE) ** -0.5 * LOG2E
    for hd in range(MLA_HEADS):
        sl = slice(hd * HEAD_SLOT, (hd + 1) * HEAD_SLOT)
        q_ref[:, sl] = (rope(q[:, sl]) * q_scale).astype(BF16)
        k_ref[:, sl] = (kn[:, sl] + kpe_rot).astype(BF16)
    vt_ref[...] = lax.dot_general(wvt_ref[...], ckv, NT_DIMS,
                                  preferred_element_type=F32).astype(BF16)
    p2 = jnp.dot(h, w2_ref[...], preferred_element_type=F32)
    nw = NA_HEADS * NA_HD
    naq_ref[...] = (p2[:, :nw] * (NA_HD ** -0.5 * LOG2E)).astype(BF16)
    nak_ref[...] = p2[:, nw:2 * nw].astype(BF16)
    nav_ref[...] = p2[:, 2 * nw:].astype(BF16)
    p3 = jnp.dot(h, w3_ref[...], preferred_element_type=F32) + bgate_ref[...]
    gate_ref[...] = jax.nn.sigmoid(p3).astype(BF16)


def _in_proj(x, prm, tm):
    B, S, _ = x.shape
    nt = S // tm
    nw = NA_HEADS * NA_HD
    qw = MLA_HEADS * HEAD_SLOT
    tok = lambda w: pl.BlockSpec((None, tm, w), lambda b, i: (b, i, 0))
    rope_spec = pl.BlockSpec((tm, HEAD_SLOT), lambda b, i: (i, 0))
    consts = [prm["gmix"], prm["w1"], prm["w2"], prm["w3"], prm["bgate"], prm["gq"], prm["wqb"],
              prm["gkv"], prm["wk"], prm["wvt"]]
    cos, sina, sinb = prm["rope"](S)
    out_shape = (
        jax.ShapeDtypeStruct((B, S, qw), BF16), jax.ShapeDtypeStruct((B, S, qw), BF16),
        jax.ShapeDtypeStruct((B, MLA_HEADS * V_HEAD, S), BF16),
        jax.ShapeDtypeStruct((B, S, nw), BF16), jax.ShapeDtypeStruct((B, S, nw), BF16),
        jax.ShapeDtypeStruct((B, S, nw), BF16), jax.ShapeDtypeStruct((B, S, 2 * D_MODEL), BF16))
    out_specs = (tok(qw), tok(qw),
                 pl.BlockSpec((None, MLA_HEADS * V_HEAD, tm), lambda b, i: (b, 0, i)),
                 tok(nw), tok(nw), tok(nw), tok(2 * D_MODEL))
    return pl.pallas_call(
        _in_proj_kernel, grid=(B, nt), out_shape=out_shape,
        in_specs=[tok(D_MODEL)] + [_const_spec(c.shape) for c in consts] + [rope_spec] * 3,
        out_specs=out_specs, name="in_proj",
        compiler_params=pltpu.CompilerParams(
            dimension_semantics=("arbitrary", "arbitrary"), vmem_limit_bytes=VMEM_LIMIT),
    )(x, *consts, cos, sina, sinb)


def _mla_kernel(q_ref, k_ref, vt_ref, o_ref, *, seq, tq, tk, qsub):
    ones = jnp.ones((16, tk), BF16)

    def q_body(qi, carry):
        q0 = pl.multiple_of(qi * qsub, qsub)
        outs = []
        for hh in range(2):
            lanes = slice(hh * HEAD_SLOT, (hh + 1) * HEAD_SLOT)
            qh = q_ref[pl.ds(q0, qsub), lanes]

            def c_body(c, state, lanes=lanes, qh=qh, hh=hh):
                m, acc = state
                k0 = pl.multiple_of(c * tk, tk)
                kc = k_ref[pl.ds(k0, tk), lanes]
                st = lax.dot_general(kc, qh, NT_DIMS, preferred_element_type=F32)
                m_new = jnp.maximum(m, jnp.max(st, axis=0, keepdims=True))
                alpha = jnp.exp2(m - m_new)
                pt = jnp.exp2((st - m_new).astype(BF16))
                vc = jnp.concatenate(
                    [vt_ref[hh * V_HEAD:(hh + 1) * V_HEAD, pl.ds(k0, tk)], ones], axis=0)
                acc = acc * alpha + jnp.dot(vc, pt, preferred_element_type=F32)
                return m_new, acc

            m0 = jnp.full((1, qsub), NEG, F32)
            acc0 = jnp.zeros((V_HEAD + 16, qsub), F32)
            _, acc = lax.fori_loop(0, seq // tk, c_body, (m0, acc0))
            outs.append(acc[:V_HEAD] / acc[V_HEAD:V_HEAD + 1])
        ot = jnp.concatenate(outs, axis=0)
        o_ref[pl.ds(q0, qsub), :] = ot.T.astype(BF16)
        return carry

    lax.fori_loop(0, tq // qsub, q_body, 0)


def _mla_attn(q, k, vt, tq, tk, qsub):
    B, S, _ = q.shape
    pairs = MLA_HEADS // 2
    return pl.pallas_call(
        functools.partial(_mla_kernel, seq=S, tq=tq, tk=tk, qsub=qsub),
        grid=(B, pairs, S // tq),
        out_shape=jax.ShapeDtypeStruct((B, S, MLA_HEADS * V_HEAD), BF16),
        in_specs=[pl.BlockSpec((None, tq, 2 * HEAD_SLOT), lambda b, p, i: (b, i, p)),
                  pl.BlockSpec((None, S, 2 * HEAD_SLOT), lambda b, p, i: (b, 0, p)),
                  pl.BlockSpec((None, 2 * V_HEAD, S), lambda b, p, i: (b, p, 0))],
        out_specs=pl.BlockSpec((None, tq, 2 * V_HEAD), lambda b, p, i: (b, i, p)),
        name="mla_attn",
        compiler_params=pltpu.CompilerParams(
            dimension_semantics=("arbitrary", "arbitrary", "arbitrary"),
            vmem_limit_bytes=VMEM_LIMIT),
    )(q, k, vt)


def _na_row_params(kind, j):
    if kind == "first":
        return 0, NA_WIN_R - 1 - j
    if kind == "last":
        return NA_QROWS, -1 - j
    return j, NA_WIN_R // 2 - 1 - j


def _na_kernel(q_ref, k0_ref, k1_ref, k2_ref, v0_ref, v1_ref, v2_ref, f_ref, o_ref, *, nblk):
    i = pl.program_id(1)
    kwin = jnp.concatenate([k0_ref[...], k1_ref[...], k2_ref[...]], axis=0)
    vwin = jnp.concatenate([v0_ref[...], v1_ref[...], v2_ref[...]], axis=0)
    lane_q = lax.broadcasted_iota(jnp.int32, (NA_BLK, LANES), 1)
    lane_w = lax.broadcasted_iota(jnp.int32, (GRID_W, LANES), 1)
    pairs_per_slab = NA_WROWS // 2

    def compute(kind):
        for hp in range(NA_HEADS // 2):
            lanes = slice(hp * LANES, (hp + 1) * LANES)
            qp, kp, vp = q_ref[:, lanes], kwin[:, lanes], vwin[:, lanes]
            o_pair = None
            for hh in range(2):
                head = 2 * hp + hh
                own = (lane_q < NA_HD) if hh == 0 else (lane_q >= NA_HD)
                qm = jnp.where(own, qp, jnp.zeros_like(qp))
                s = lax.dot_general(qm, kp, NT_DIMS, preferred_element_type=F32)
                p_rows, linv_rows = [], []
                for j in range(NA_QROWS):
                    off, dbase = _na_row_params(kind, j)
                    blocks = {}
                    for b in range(pairs_per_slab):
                        lo_ok = off <= 2 * b < off + NA_WIN_R
                        hi_ok = off <= 2 * b + 1 < off + NA_WIN_R
                        if not (lo_ok or hi_ok):
                            continue
                        d = 2 * b + dbase
                        sb = (s[j * GRID_W:(j + 1) * GRID_W, b * LANES:(b + 1) * LANES]
                              + f_ref[head, d + 1])
                        if not lo_ok:
                            sb = jnp.where(lane_w >= GRID_W, sb, NEG)
                        if not hi_ok:
                            sb = jnp.where(lane_w < GRID_W, sb, NEG)
                        blocks[b] = sb
                    m = jnp.max(functools.reduce(jnp.maximum, blocks.values()),
                                axis=1, keepdims=True)
                    ps = {b: jnp.exp2(sb - m) for b, sb in blocks.items()}
                    l = jnp.sum(functools.reduce(jnp.add, ps.values()), axis=1, keepdims=True)
                    zero = jnp.zeros((GRID_W, LANES), BF16)
                    p_rows.append(jnp.concatenate(
                        [ps[b].astype(BF16) if b in ps else zero for b in range(pairs_per_slab)],
                        axis=1))
                    linv_rows.append(1.0 / l)
                p = jnp.concatenate(p_rows, axis=0)
                o = jnp.dot(p, vp, preferred_element_type=F32) * jnp.concatenate(linv_rows, axis=0)
                o_pair = o if hh == 0 else jnp.where(lane_q < NA_HD, o_pair, o)
            o_ref[:, lanes] = o_pair.astype(BF16)

    pl.when(i == 0)(lambda: compute("first"))
    pl.when(jnp.logical_and(i > 0, i < nblk - 1))(lambda: compute("mid"))
    pl.when(i == nblk - 1)(lambda: compute("last"))


def _na_bias_table(rpb):
    w = jnp.arange(GRID_W)[:, None]
    kc = jnp.arange(GRID_W)[None, :]
    cs = jnp.clip(w - NA_WIN_C // 2, 0, GRID_W - NA_WIN_C)
    valid = (kc >= cs) & (kc < cs + NA_WIN_C)
    dc = jnp.clip(kc - w + NA_WIN_C - 1, 0, 2 * NA_WIN_C - 2)
    m = jnp.where(valid, rpb[:, :, dc] * LOG2E, NEG)
    neg = jnp.full((NA_HEADS, 1, GRID_W, GRID_W), NEG, F32)
    mext = jnp.concatenate([neg, m, neg], axis=1)
    n = 2 * NA_WIN_R
    return jnp.concatenate([mext[:, :n], mext[:, 1:n + 1]], axis=-1)


def _na_attn(q, k, v, ftab):
    B, S, W = q.shape
    nblk = S // NA_BLK
    nwin = NA_WROWS // NA_QROWS

    def win_spec(t):
        return pl.BlockSpec((None, NA_BLK, W),
                            lambda b, i: (b, jnp.clip(i - 1, 0, nblk - nwin) + t, 0))

    blk = pl.BlockSpec((None, NA_BLK, W), lambda b, i: (b, i, 0))
    return pl.pallas_call(
        functools.partial(_na_kernel, nblk=nblk), grid=(B, nblk),
        out_shape=jax.ShapeDtypeStruct((B, S, W), BF16),
        in_specs=[blk] + [win_spec(t) for t in range(nwin)] * 2 + [_const_spec(ftab.shape)],
        out_specs=blk, name="na_attn",
        compiler_params=pltpu.CompilerParams(
            dimension_semantics=("arbitrary", "arbitrary"), vmem_limit_bytes=VMEM_LIMIT),
    )(q, k, k, k, v, v, v, ftab)


def _merge_kernel(x_ref, oa_ref, ob_ref, gate_ref, wa_ref, wb_ref, wo_ref, gffn_ref,
                  x1_ref, h2_ref):
    a = jnp.dot(oa_ref[...], wa_ref[...], preferred_element_type=F32)
    b = jnp.dot(ob_ref[...], wb_ref[...], preferred_element_type=F32)
    merged = gate_ref[:, :D_MODEL].astype(F32) * a + gate_ref[:, D_MODEL:].astype(F32) * b
    x1 = x_ref[...] + jnp.dot(merged.astype(BF16), wo_ref[...], preferred_element_type=F32)
    x1_ref[...] = x1
    h2_ref[...] = _rms(x1, gffn_ref[...]).astype(BF16)


def _merge(x, oa, ob, gates, prm, tm):
    B, S, _ = x.shape
    tok = lambda w: pl.BlockSpec((None, tm, w), lambda b, i: (b, i, 0))
    consts = [prm["wa"], prm["wb"], prm["wo"], prm["gffn"]]
    return pl.pallas_call(
        _merge_kernel, grid=(B, S // tm),
        out_shape=(jax.ShapeDtypeStruct((B, S, D_MODEL), F32),
                   jax.ShapeDtypeStruct((B, S, D_MODEL), BF16)),
        in_specs=[tok(D_MODEL), tok(oa.shape[-1]), tok(ob.shape[-1]), tok(2 * D_MODEL)]
        + [_const_spec(c.shape) for c in consts],
        out_specs=(tok(D_MODEL), tok(D_MODEL)), name="merge",
        compiler_params=pltpu.CompilerParams(
            dimension_semantics=("arbitrary", "arbitrary"), vmem_limit_bytes=VMEM_LIMIT),
    )(x, oa, ob, gates, *consts)


FFN_HALO = 16
FFN_CHUNKS = ((0, 1024), (1024, 1024), (2048, 768))


def _ffn_kernel(x1_ref, h_ref, hprev_ref, hnext_ref, wu_ref, wg_ref, cw_ref, cb_ref, wd_ref,
                gfin_ref, o_ref, *, nt):
    i = pl.program_id(1)
    tm = h_ref.shape[0]
    h = h_ref[...]
    hprev = jnp.where(i > 0, hprev_ref[...], jnp.zeros_like(hprev_ref))
    hnext = jnp.where(i < nt - 1, hnext_ref[...], jnp.zeros_like(hnext_ref))
    hext = jnp.concatenate([hprev, h, hnext], axis=0)
    y = jnp.zeros((tm, D_MODEL), F32)
    for c0, cw in FFN_CHUNKS:
        cols = slice(c0, c0 + cw)
        u = jnp.dot(h, wu_ref[:, cols], preferred_element_type=F32)
        g = jnp.dot(hext, wg_ref[:, cols], preferred_element_type=F32)
        conv = (g[FFN_HALO - 1:FFN_HALO - 1 + tm] * cw_ref[0:1, cols]
                + g[FFN_HALO:FFN_HALO + tm] * cw_ref[1:2, cols]
                + g[FFN_HALO + 1:FFN_HALO + 1 + tm] * cw_ref[2:3, cols] + cb_ref[:, cols])
        act = (jax.nn.silu(conv) * u).astype(BF16)
        y = y + jnp.dot(act, wd_ref[cols, :], preferred_element_type=F32)
    o_ref[...] = _rms(x1_ref[...] + y, gfin_ref[...])


def _conv_ffn(x1, h2, prm, tm):
    B, S, _ = x1.shape
    nt = S // tm
    hb = tm // FFN_HALO
    tok = pl.BlockSpec((None, tm, D_MODEL), lambda b, i: (b, i, 0))
    prev = pl.BlockSpec((None, FFN_HALO, D_MODEL), lambda b, i: (b, jnp.maximum(i * hb - 1, 0), 0))
    nxt = pl.BlockSpec((None, FFN_HALO, D_MODEL),
                       lambda b, i: (b, jnp.minimum((i + 1) * hb, S // FFN_HALO - 1), 0))
    consts = [prm["wu"], prm["wg"], prm["cw"], prm["cb"], prm["wd"], prm["gfin"]]
    return pl.pallas_call(
        functools.partial(_ffn_kernel, nt=nt), grid=(B, nt),
        out_shape=jax.ShapeDtypeStruct((B, S, D_MODEL), F32),
        in_specs=[tok, tok, prev, nxt] + [_const_spec(c.shape) for c in consts],
        out_specs=tok, name="conv_ffn",
        compiler_params=pltpu.CompilerParams(
            dimension_semantics=("arbitrary", "arbitrary"), vmem_limit_bytes=VMEM_LIMIT),
    )(x1, h2, h2, h2, *consts)


def _rope_tables(S):
    half = QK_ROPE // 2
    inv = ROPE_BASE ** (-jnp.arange(0, QK_ROPE, 2, dtype=F32) / QK_ROPE)
    ang = jnp.arange(S, dtype=F32)[:, None] * inv[None, :]
    cos, sin = jnp.cos(ang), jnp.sin(ang)
    z = lambda n: jnp.zeros((S, n), F32)
    tail = HEAD_SLOT - QK_NOPE - QK_ROPE
    cos_t = jnp.concatenate([jnp.ones((S, QK_NOPE), F32), cos, cos, z(tail)], axis=1)
    sina = jnp.concatenate([z(QK_NOPE), -sin, z(half), z(tail)], axis=1)
    sinb = jnp.concatenate([z(QK_NOPE), z(half), sin, z(tail)], axis=1)
    return cos_t, sina, sinb


def _prep_params(norm_mix, w_in, b_gate, norm_q, w_qb, norm_kv, w_kvb, rpb, w_br_a, w_br_b, w_o,
                 norm_ffn, w_up, conv_w, conv_b, w_down, norm_final):
    nw = NA_HEADS * NA_HD
    c_kpe = Q_LORA + KV_LORA
    c_na = c_kpe + QK_ROPE
    c_gate = c_na + 3 * nw
    kpe_slab = jnp.zeros((D_MODEL, HEAD_SLOT), F32).at[:, QK_NOPE:QK_NOPE + QK_ROPE].set(
        w_in[:, c_kpe:c_na])
    w1 = jnp.concatenate([w_in[:, :c_kpe], kpe_slab], axis=1)
    qhd = QK_NOPE + QK_ROPE
    wqb = jnp.pad(w_qb.reshape(Q_LORA, MLA_HEADS, qhd), ((0, 0), (0, 0), (0, HEAD_SLOT - qhd)))
    wkv = w_kvb.reshape(KV_LORA, MLA_HEADS, QK_NOPE + V_HEAD)
    wk = jnp.pad(wkv[:, :, :QK_NOPE], ((0, 0), (0, 0), (0, HEAD_SLOT - QK_NOPE)))
    wvt = wkv[:, :, QK_NOPE:].reshape(KV_LORA, MLA_HEADS * V_HEAD).T
    row = lambda v: v.reshape(1, -1).astype(F32)
    return dict(
        gmix=row(norm_mix), w1=w1.astype(BF16), w2=w_in[:, c_na:c_gate].astype(BF16),
        w3=w_in[:, c_gate:].astype(BF16), bgate=row(b_gate), gq=row(norm_q),
        wqb=wqb.reshape(Q_LORA, MLA_HEADS * HEAD_SLOT).astype(BF16), gkv=row(norm_kv),
        wk=wk.reshape(KV_LORA, MLA_HEADS * HEAD_SLOT).astype(BF16), wvt=wvt.astype(BF16),
        rope=_rope_tables, ftab=_na_bias_table(rpb),
        wa=w_br_a.astype(BF16), wb=w_br_b.astype(BF16), wo=w_o.astype(BF16), gffn=row(norm_ffn),
        wu=w_up[:, :D_FF].astype(BF16), wg=w_up[:, D_FF:].astype(BF16), cw=conv_w.astype(F32),
        cb=row(conv_b), wd=w_down.astype(BF16), gfin=row(norm_final))


def _tiles(S):
    tm = min(512, S)
    return dict(tm=tm, tq=min(1024, S), tk=min(1024, S), qsub=min(256, S))


def _trunk(x, prm):
    S = x.shape[1]
    t = _tiles(S)
    q, k, vt, naq, nak, nav, gates = _in_proj(x, prm, t["tm"])
    oa = _mla_attn(q, k, vt, t["tq"], t["tk"], t["qsub"])
    ob = _na_attn(naq, nak, nav, prm["ftab"])
    x1, h2 = _merge(x, oa, ob, gates, prm, t["tm"])
    return _conv_ffn(x1, h2, prm, t["tm"])


def kernel(x_prompt, x_sample, norm_mix, w_in, b_gate, norm_q, w_qb, norm_kv, w_kvb, rpb, w_br_a,
           w_br_b, w_o, norm_ffn, w_up, conv_w, conv_b, w_down, norm_final):
    prm = _prep_params(norm_mix[0], w_in[0], b_gate[0], norm_q[0], w_qb[0], norm_kv[0], w_kvb[0],
                       rpb[0], w_br_a[0], w_br_b[0], w_o[0], norm_ffn[0], w_up[0], conv_w[0],
                       conv_b[0], w_down[0], norm_final)
    return _trunk(x_prompt, prm), _trunk(x_sample, prm)
```

```python
import functools
import math

import jax
import jax.numpy as jnp
import numpy as np
from jax import lax
from jax.experimental import pallas as pl
from jax.experimental.pallas import tpu as pltpu

D_MODEL = 1024
GRID_W = 64
MLA_HEADS = 8
QK_NOPE = 64
QK_ROPE = 32
V_HEAD = 64
Q_LORA = 768
KV_LORA = 256
ROPE_BASE = 10000.0
NA_HEADS = 8
NA_HD = 64
NA_WIN_R = 8
NA_WIN_C = 16
D_FF = 2816
EPS = 1e-6

LANES = 128
HEAD_SLOT = 128
VT_ROWS = 80
LOG2E = 1.4426950408889634
NEG = -1e30
VMEM_LIMIT = 56 * 1024 * 1024

NA_QROWS = 4
NA_WROWS = NA_QROWS + NA_WIN_R
NA_BLK = NA_QROWS * GRID_W

BF16 = jnp.bfloat16
F32 = jnp.float32
NT_DIMS = (((1,), (1,)), ((), ()))


def _rms(x, g):
    return x * lax.rsqrt(jnp.mean(x * x, axis=-1, keepdims=True) + EPS) * g


def _const_spec(shape):
    zeros = (0,) * len(shape)
    return pl.BlockSpec(shape, lambda *_: zeros, pipeline_mode=pl.Buffered(1))


def _in_proj_kernel(x_ref, gmix_ref, w1_ref, w2_ref, w3_ref, bgate_ref, gq_ref, wqb_ref,
                    gkv_ref, wk_ref, wvt_ref, cos_ref, sina_ref, sinb_ref,
                    q_ref, k_ref, vt_ref, naq_ref, nak_ref, nav_ref, gate_ref):
    h = _rms(x_ref[...], gmix_ref[...]).astype(BF16)
    p1 = jnp.dot(h, w1_ref[...], preferred_element_type=F32)
    cq = _rms(p1[:, :Q_LORA], gq_ref[...]).astype(BF16)
    ckv = _rms(p1[:, Q_LORA:Q_LORA + KV_LORA], gkv_ref[...]).astype(BF16)
    kpe = p1[:, Q_LORA + KV_LORA:]
    cos, sina, sinb = cos_ref[...], sina_ref[...], sinb_ref[...]

    def rope(slab):
        return (slab * cos + pltpu.roll(slab, HEAD_SLOT - QK_ROPE // 2, 1) * sina
                + pltpu.roll(slab, QK_ROPE // 2, 1) * sinb)

    kpe_rot = rope(kpe)
    q = jnp.dot(cq, wqb_ref[...], preferred_element_type=F32)
    kn = jnp.dot(ckv, wk_ref[...], preferred_element_type=F32)
    q_scale = (QK_NOPE + QK_ROPE) ** -0.5 * LOG2E
    for hd in range(MLA_HEADS):
        sl = slice(hd * HEAD_SLOT, (hd + 1) * HEAD_SLOT)
        q_ref[:, sl] = (rope(q[:, sl]) * q_scale).astype(BF16)
        k_ref[:, sl] = (kn[:, sl] + kpe_rot).astype(BF16)
    vt = lax.dot_general(wvt_ref[...], ckv, NT_DIMS,
                         preferred_element_type=F32).astype(BF16)
    ones = jnp.ones((VT_ROWS - V_HEAD, vt.shape[1]), BF16)
    for hd in range(MLA_HEADS):
        vt_ref[hd * VT_ROWS:hd * VT_ROWS + V_HEAD, :] = vt[hd * V_HEAD:(hd + 1) * V_HEAD, :]
        vt_ref[hd * VT_ROWS + V_HEAD:(hd + 1) * VT_ROWS, :] = ones
    p2 = jnp.dot(h, w2_ref[...], preferred_element_type=F32)
    nw = NA_HEADS * NA_HD
    naq_ref[...] = (p2[:, :nw] * (NA_HD ** -0.5 * LOG2E)).astype(BF16)
    nak_ref[...] = p2[:, nw:2 * nw].astype(BF16)
    nav_ref[...] = p2[:, 2 * nw:].astype(BF16)
    p3 = jnp.dot(h, w3_ref[...], preferred_element_type=F32) + bgate_ref[...]
    gate_ref[...] = jax.nn.sigmoid(p3).astype(BF16)


def _in_proj(x, prm, tm):
    B, S, _ = x.shape
    nt = S // tm
    nw = NA_HEADS * NA_HD
    qw = MLA_HEADS * HEAD_SLOT
    tok = lambda w: pl.BlockSpec((None, tm, w), lambda b, i: (b, i, 0))
    rope_spec = pl.BlockSpec((tm, HEAD_SLOT), lambda b, i: (i, 0))
    consts = [prm["gmix"], prm["w1"], prm["w2"], prm["w3"], prm["bgate"], prm["gq"], prm["wqb"],
              prm["gkv"], prm["wk"], prm["wvt"]]
    cos, sina, sinb = prm["rope"](S)
    out_shape = (
        jax.ShapeDtypeStruct((B, S, qw), BF16), jax.ShapeDtypeStruct((B, S, qw), BF16),
        jax.ShapeDtypeStruct((B, MLA_HEADS * VT_ROWS, S), BF16),
        jax.ShapeDtypeStruct((B, S, nw), BF16), jax.ShapeDtypeStruct((B, S, nw), BF16),
        jax.ShapeDtypeStruct((B, S, nw), BF16), jax.ShapeDtypeStruct((B, S, 2 * D_MODEL), BF16))
    out_specs = (tok(qw), tok(qw),
                 pl.BlockSpec((None, MLA_HEADS * VT_ROWS, tm), lambda b, i: (b, 0, i)),
                 tok(nw), tok(nw), tok(nw), tok(2 * D_MODEL))
    return pl.pallas_call(
        _in_proj_kernel, grid=(B, nt), out_shape=out_shape,
        in_specs=[tok(D_MODEL)] + [_const_spec(c.shape) for c in consts] + [rope_spec] * 3,
        out_specs=out_specs, name="in_proj",
        compiler_params=pltpu.CompilerParams(
            dimension_semantics=("arbitrary", "arbitrary"), vmem_limit_bytes=VMEM_LIMIT),
    )(x, *consts, cos, sina, sinb)


def _mla_kernel(q_ref, k_ref, vt_ref, o_ref, sa_ref, sb_ref, *, seq, tq, tk, qsub):
    nc = seq // tk
    lanes = [slice(hh * HEAD_SLOT, (hh + 1) * HEAD_SLOT) for hh in range(2)]

    def scores(c, qh, hh):
        k0 = pl.multiple_of(c * tk, tk)
        return lax.dot_general(k_ref[pl.ds(k0, tk), lanes[hh]], qh, NT_DIMS,
                               preferred_element_type=F32)

    def consume(c, s_ref, state, hh):
        m, acc = state
        m_new = jnp.maximum(m, jnp.max(s_ref[hh], axis=0, keepdims=True))
        alpha = jnp.exp2(m - m_new)
        pt = jnp.exp2((s_ref[hh] - m_new).astype(BF16))
        k0 = pl.multiple_of(c * tk, tk)
        vc = vt_ref[hh * VT_ROWS:(hh + 1) * VT_ROWS, pl.ds(k0, tk)]
        return m_new, acc * alpha + jnp.dot(vc, pt, preferred_element_type=F32)

    nq = tq // qsub

    def load_q(qi):
        q0 = pl.multiple_of(qi * qsub, qsub)
        return [q_ref[pl.ds(q0, qsub), lanes[hh]] for hh in range(2)]

    qfirst = load_q(0)
    for hh in range(2):
        sa_ref[hh] = scores(0, qfirst[hh], hh)

    def q_body(qi, carry):
        q0 = pl.multiple_of(qi * qsub, qsub)
        qh = load_q(qi)
        qnext = load_q(jnp.minimum(qi + 1, nq - 1))

        def pair_body(i, state):
            state = list(state)
            c = 2 * i
            for hh in range(2):
                sb_ref[hh] = scores(c + 1, qh[hh], hh)
                state[hh] = consume(c, sa_ref, state[hh], hh)
            for hh in range(2):
                sa_ref[hh] = scores(c + 2, qh[hh], hh)
                state[hh] = consume(c + 1, sb_ref, state[hh], hh)
            return tuple(state)

        init = tuple((jnp.full((1, qsub), NEG, F32), jnp.zeros((V_HEAD + 16, qsub), F32))
                     for _ in range(2))
        state = list(lax.fori_loop(0, nc // 2 - 1, pair_body, init))
        for hh in range(2):
            sb_ref[hh] = scores(nc - 1, qh[hh], hh)
            state[hh] = consume(nc - 2, sa_ref, state[hh], hh)
        outs = []
        for hh in range(2):
            sa_ref[hh] = scores(0, qnext[hh], hh)
            _, acc = consume(nc - 1, sb_ref, state[hh], hh)
            outs.append(acc[:V_HEAD] / acc[V_HEAD:V_HEAD + 1])
        ot = jnp.concatenate(outs, axis=0)
        o_ref[pl.ds(q0, qsub), :] = ot.T.astype(BF16)
        return carry

    lax.fori_loop(0, nq, q_body, 0)


def _mla_attn(q, k, vt, tq, tk, qsub):
    B, S, _ = q.shape
    pairs = MLA_HEADS // 2
    return pl.pallas_call(
        functools.partial(_mla_kernel, seq=S, tq=tq, tk=tk, qsub=qsub),
        grid=(B, pairs, S // tq),
        out_shape=jax.ShapeDtypeStruct((B, S, MLA_HEADS * V_HEAD), BF16),
        in_specs=[pl.BlockSpec((None, tq, 2 * HEAD_SLOT), lambda b, p, i: (b, i, p)),
                  pl.BlockSpec((None, S, 2 * HEAD_SLOT), lambda b, p, i: (b, 0, p)),
                  pl.BlockSpec((None, 2 * VT_ROWS, S), lambda b, p, i: (b, p, 0))],
        out_specs=pl.BlockSpec((None, tq, 2 * V_HEAD), lambda b, p, i: (b, i, p)),
        scratch_shapes=[pltpu.VMEM((2, tk, qsub), F32), pltpu.VMEM((2, tk, qsub), F32)],
        name="mla_attn",
        compiler_params=pltpu.CompilerParams(
            dimension_semantics=("arbitrary", "arbitrary", "arbitrary"),
            vmem_limit_bytes=VMEM_LIMIT),
    )(q, k, vt)


def _na_row_params(kind, j):
    if kind == "first":
        return 0, NA_WIN_R - 1 - j
    if kind == "last":
        return NA_QROWS, -1 - j
    return j, NA_WIN_R // 2 - 1 - j


def _na_kernel(q_ref, k0_ref, k1_ref, k2_ref, v0_ref, v1_ref, v2_ref, f_ref, o_ref, *, nblk):
    i = pl.program_id(1)
    kwin = jnp.concatenate([k0_ref[...], k1_ref[...], k2_ref[...]], axis=0)
    vwin = jnp.concatenate([v0_ref[...], v1_ref[...], v2_ref[...]], axis=0)
    lane_q = lax.broadcasted_iota(jnp.int32, (NA_BLK, LANES), 1)
    lane_w = lax.broadcasted_iota(jnp.int32, (GRID_W, LANES), 1)
    pairs_per_slab = NA_WROWS // 2

    def compute(kind):
        for hp in range(NA_HEADS // 2):
            lanes = slice(hp * LANES, (hp + 1) * LANES)
            qp, kp, vp = q_ref[:, lanes], kwin[:, lanes], vwin[:, lanes]
            o_pair = None
            for hh in range(2):
                head = 2 * hp + hh
                own = (lane_q < NA_HD) if hh == 0 else (lane_q >= NA_HD)
                qm = jnp.where(own, qp, jnp.zeros_like(qp))
                s = lax.dot_general(qm, kp, NT_DIMS, preferred_element_type=F32)
                p_rows, linv_rows = [], []
                for j in range(NA_QROWS):
                    off, dbase = _na_row_params(kind, j)
                    blocks = {}
                    for b in range(pairs_per_slab):
                        lo_ok = off <= 2 * b < off + NA_WIN_R
                        hi_ok = off <= 2 * b + 1 < off + NA_WIN_R
                        if not (lo_ok or hi_ok):
                            continue
                        d = 2 * b + dbase
                        sb = (s[j * GRID_W:(j + 1) * GRID_W, b * LANES:(b + 1) * LANES]
                              + f_ref[head, d + 1])
                        if not lo_ok:
                            sb = jnp.where(lane_w >= GRID_W, sb, NEG)
                        if not hi_ok:
                            sb = jnp.where(lane_w < GRID_W, sb, NEG)
                        blocks[b] = sb
                    m = jnp.max(functools.reduce(jnp.maximum, blocks.values()),
                                axis=1, keepdims=True)
                    ps = {b: jnp.exp2(sb - m) for b, sb in blocks.items()}
                    l = jnp.sum(functools.reduce(jnp.add, ps.values()), axis=1, keepdims=True)
                    zero = jnp.zeros((GRID_W, LANES), BF16)
                    p_rows.append(jnp.concatenate(
                        [ps[b].astype(BF16) if b in ps else zero for b in range(pairs_per_slab)],
                        axis=1))
                    linv_rows.append(1.0 / l)
                p = jnp.concatenate(p_rows, axis=0)
                o = jnp.dot(p, vp, preferred_element_type=F32) * jnp.concatenate(linv_rows, axis=0)
                o_pair = o if hh == 0 else jnp.where(lane_q < NA_HD, o_pair, o)
            o_ref[:, lanes] = o_pair.astype(BF16)

    pl.when(i == 0)(lambda: compute("first"))
    pl.when(jnp.logical_and(i > 0, i < nblk - 1))(lambda: compute("mid"))
    pl.when(i == nblk - 1)(lambda: compute("last"))


def _na_bias_table(rpb):
    w = jnp.arange(GRID_W)[:, None]
    kc = jnp.arange(GRID_W)[None, :]
    cs = jnp.clip(w - NA_WIN_C // 2, 0, GRID_W - NA_WIN_C)
    valid = (kc >= cs) & (kc < cs + NA_WIN_C)
    dc = jnp.clip(kc - w + NA_WIN_C - 1, 0, 2 * NA_WIN_C - 2)
    m = jnp.where(valid, rpb[:, :, dc] * LOG2E, NEG)
    neg = jnp.full((NA_HEADS, 1, GRID_W, GRID_W), NEG, F32)
    mext = jnp.concatenate([neg, m, neg], axis=1)
    n = 2 * NA_WIN_R
    return jnp.concatenate([mext[:, :n], mext[:, 1:n + 1]], axis=-1)


def _na_attn(q, k, v, ftab):
    B, S, W = q.shape
    nblk = S // NA_BLK
    nwin = NA_WROWS // NA_QROWS

    def win_spec(t):
        return pl.BlockSpec((None, NA_BLK, W),
                            lambda b, i: (b, jnp.clip(i - 1, 0, nblk - nwin) + t, 0))

    blk = pl.BlockSpec((None, NA_BLK, W), lambda b, i: (b, i, 0))
    return pl.pallas_call(
        functools.partial(_na_kernel, nblk=nblk), grid=(B, nblk),
        out_shape=jax.ShapeDtypeStruct((B, S, W), BF16),
        in_specs=[blk] + [win_spec(t) for t in range(nwin)] * 2 + [_const_spec(ftab.shape)],
        out_specs=blk, name="na_attn",
        compiler_params=pltpu.CompilerParams(
            dimension_semantics=("arbitrary", "arbitrary"), vmem_limit_bytes=VMEM_LIMIT),
    )(q, k, k, k, v, v, v, ftab)


def _merge_kernel(x_ref, oa_ref, ob_ref, gate_ref, wa_ref, wb_ref, wo_ref, gffn_ref,
                  x1_ref, h2_ref):
    a = jnp.dot(oa_ref[...], wa_ref[...], preferred_element_type=F32)
    b = jnp.dot(ob_ref[...], wb_ref[...], preferred_element_type=F32)
    merged = gate_ref[:, :D_MODEL].astype(F32) * a + gate_ref[:, D_MODEL:].astype(F32) * b
    x1 = x_ref[...] + jnp.dot(merged.astype(BF16), wo_ref[...], preferred_element_type=F32)
    x1_ref[...] = x1
    h2_ref[...] = _rms(x1, gffn_ref[...]).astype(BF16)


def _merge(x, oa, ob, gates, prm, tm):
    B, S, _ = x.shape
    tok = lambda w: pl.BlockSpec((None, tm, w), lambda b, i: (b, i, 0))
    consts = [prm["wa"], prm["wb"], prm["wo"], prm["gffn"]]
    return pl.pallas_call(
        _merge_kernel, grid=(B, S // tm),
        out_shape=(jax.ShapeDtypeStruct((B, S, D_MODEL), F32),
                   jax.ShapeDtypeStruct((B, S, D_MODEL), BF16)),
        in_specs=[tok(D_MODEL), tok(oa.shape[-1]), tok(ob.shape[-1]), tok(2 * D_MODEL)]
        + [_const_spec(c.shape) for c in consts],
        out_specs=(tok(D_MODEL), tok(D_MODEL)), name="merge",
        compiler_params=pltpu.CompilerParams(
            dimension_semantics=("arbitrary", "arbitrary"), vmem_limit_bytes=VMEM_LIMIT),
    )(x, oa, ob, gates, *consts)


FFN_HALO = 16
FFN_CHUNKS = ((0, 1024), (1024, 1024), (2048, 768))


def _ffn_kernel(x1_ref, h_ref, hprev_ref, hnext_ref, wu_ref, wg_ref, cw_ref, cb_ref, wd_ref,
                gfin_ref, o_ref, *, nt):
    i = pl.program_id(1)
    tm = h_ref.shape[0]
    h = h_ref[...]
    hprev = jnp.where(i > 0, hprev_ref[...], jnp.zeros_like(hprev_ref))
    hnext = jnp.where(i < nt - 1, hnext_ref[...], jnp.zeros_like(hnext_ref))
    hext = jnp.concatenate([hprev, h, hnext], axis=0)
    y = jnp.zeros((tm, D_MODEL), F32)
    for c0, cw in FFN_CHUNKS:
        cols = slice(c0, c0 + cw)
        u = jnp.dot(h, wu_ref[:, cols], preferred_element_type=F32)
        g = jnp.dot(hext, wg_ref[:, cols], preferred_element_type=F32)
        conv = (g[FFN_HALO - 1:FFN_HALO - 1 + tm] * cw_ref[0:1, cols]
                + g[FFN_HALO:FFN_HALO + tm] * cw_ref[1:2, cols]
                + g[FFN_HALO + 1:FFN_HALO + 1 + tm] * cw_ref[2:3, cols] + cb_ref[:, cols])
        act = (jax.nn.silu(conv) * u).astype(BF16)
        y = y + jnp.dot(act, wd_ref[cols, :], preferred_element_type=F32)
    o_ref[...] = _rms(x1_ref[...] + y, gfin_ref[...])


def _conv_ffn(x1, h2, prm, tm):
    B, S, _ = x1.shape
    nt = S // tm
    hb = tm // FFN_HALO
    tok = pl.BlockSpec((None, tm, D_MODEL), lambda b, i: (b, i, 0))
    prev = pl.BlockSpec((None, FFN_HALO, D_MODEL), lambda b, i: (b, jnp.maximum(i * hb - 1, 0), 0))
    nxt = pl.BlockSpec((None, FFN_HALO, D_MODEL),
                       lambda b, i: (b, jnp.minimum((i + 1) * hb, S // FFN_HALO - 1), 0))
    consts = [prm["wu"], prm["wg"], prm["cw"], prm["cb"], prm["wd"], prm["gfin"]]
    return pl.pallas_call(
        functools.partial(_ffn_kernel, nt=nt), grid=(B, nt),
        out_shape=jax.ShapeDtypeStruct((B, S, D_MODEL), F32),
        in_specs=[tok, tok, prev, nxt] + [_const_spec(c.shape) for c in consts],
        out_specs=tok, name="conv_ffn",
        compiler_params=pltpu.CompilerParams(
            dimension_semantics=("arbitrary", "arbitrary"), vmem_limit_bytes=VMEM_LIMIT),
    )(x1, h2, h2, h2, *consts)


def _rope_tables(S):
    half = QK_ROPE // 2
    inv = ROPE_BASE ** (-jnp.arange(0, QK_ROPE, 2, dtype=F32) / QK_ROPE)
    ang = jnp.arange(S, dtype=F32)[:, None] * inv[None, :]
    cos, sin = jnp.cos(ang), jnp.sin(ang)
    z = lambda n: jnp.zeros((S, n), F32)
    tail = HEAD_SLOT - QK_NOPE - QK_ROPE
    cos_t = jnp.concatenate([jnp.ones((S, QK_NOPE), F32), cos, cos, z(tail)], axis=1)
    sina = jnp.concatenate([z(QK_NOPE), -sin, z(half), z(tail)], axis=1)
    sinb = jnp.concatenate([z(QK_NOPE), z(half), sin, z(tail)], axis=1)
    return cos_t, sina, sinb


def _prep_params(norm_mix, w_in, b_gate, norm_q, w_qb, norm_kv, w_kvb, rpb, w_br_a, w_br_b, w_o,
                 norm_ffn, w_up, conv_w, conv_b, w_down, norm_final):
    nw = NA_HEADS * NA_HD
    c_kpe = Q_LORA + KV_LORA
    c_na = c_kpe + QK_ROPE
    c_gate = c_na + 3 * nw
    kpe_slab = jnp.zeros((D_MODEL, HEAD_SLOT), F32).at[:, QK_NOPE:QK_NOPE + QK_ROPE].set(
        w_in[:, c_kpe:c_na])
    w1 = jnp.concatenate([w_in[:, :c_kpe], kpe_slab], axis=1)
    qhd = QK_NOPE + QK_ROPE
    wqb = jnp.pad(w_qb.reshape(Q_LORA, MLA_HEADS, qhd), ((0, 0), (0, 0), (0, HEAD_SLOT - qhd)))
    wkv = w_kvb.reshape(KV_LORA, MLA_HEADS, QK_NOPE + V_HEAD)
    wk = jnp.pad(wkv[:, :, :QK_NOPE], ((0, 0), (0, 0), (0, HEAD_SLOT - QK_NOPE)))
    wvt = wkv[:, :, QK_NOPE:].reshape(KV_LORA, MLA_HEADS * V_HEAD).T
    row = lambda v: v.reshape(1, -1).astype(F32)
    return dict(
        gmix=row(norm_mix), w1=w1.astype(BF16), w2=w_in[:, c_na:c_gate].astype(BF16),
        w3=w_in[:, c_gate:].astype(BF16), bgate=row(b_gate), gq=row(norm_q),
        wqb=wqb.reshape(Q_LORA, MLA_HEADS * HEAD_SLOT).astype(BF16), gkv=row(norm_kv),
        wk=wk.reshape(KV_LORA, MLA_HEADS * HEAD_SLOT).astype(BF16), wvt=wvt.astype(BF16),
        rope=_rope_tables, ftab=_na_bias_table(rpb),
        wa=w_br_a.astype(BF16), wb=w_br_b.astype(BF16), wo=w_o.astype(BF16), gffn=row(norm_ffn),
        wu=w_up[:, :D_FF].astype(BF16), wg=w_up[:, D_FF:].astype(BF16), cw=conv_w.astype(F32),
        cb=row(conv_b), wd=w_down.astype(BF16), gfin=row(norm_final))


def _tiles(S):
    tm = min(512, S)
    return dict(tm=tm, tq=min(4096, S), tk=min(1024, S // 2), qsub=min(256, S))


def _trunk(x, prm):
    S = x.shape[1]
    t = _tiles(S)
    q, k, vt, naq, nak, nav, gates = _in_proj(x, prm, t["tm"])
    oa = _mla_attn(q, k, vt, t["tq"], t["tk"], t["qsub"])
    ob = _na_attn(naq, nak, nav, prm["ftab"])
    x1, h2 = _merge(x, oa, ob, gates, prm, t["tm"])
    return _conv_ffn(x1, h2, prm, t["tm"])


def kernel(x_prompt, x_sample, norm_mix, w_in, b_gate, norm_q, w_qb, norm_kv, w_kvb, rpb, w_br_a,
           w_br_b, w_o, norm_ffn, w_up, conv_w, conv_b, w_down, norm_final):
    prm = _prep_params(norm_mix[0], w_in[0], b_gate[0], norm_q[0], w_qb[0], norm_kv[0], w_kvb[0],
                       rpb[0], w_br_a[0], w_br_b[0], w_o[0], norm_ffn[0], w_up[0], conv_w[0],
                       conv_b[0], w_down[0], norm_final)
    return _trunk(x_prompt, prm), _trunk(x_sample, prm)
```
